```python
import math
import jax, jax.numpy as jnp
from jax import lax
import numpy as np

D_MODEL = 1024
BATCH = 32
SEQ = 2048
DEPTH = 1

MIX_WIDTH = D_MODEL
DIFF_WIDTH = MIX_WIDTH // 2
GLA_WIDTH = MIX_WIDTH - DIFF_WIDTH
DIFF_HEADS = 4
DIFF_HEAD_DIM = DIFF_WIDTH // (2 * DIFF_HEADS)
Q_BLOCK = 128
ROPE_THETA = 10000.0
GLA_HEADS = 4
GLA_DV = GLA_WIDTH // GLA_HEADS
GLA_DK = GLA_DV // 2
GLA_GATE_RANK = 16
GLA_GATE_TAU = 16.0
GLA_CHUNK = 64
PEER_HEADS = 8
PEER_TOPK = 16
PEER_NKEYS = 128
PEER_N_EXPERTS = PEER_NKEYS * PEER_NKEYS
PEER_QDIM = 256
PEER_TOKEN_BLOCK = 128
EPS = 1e-6

DIFF_Q_COLS = DIFF_HEADS * 2 * DIFF_HEAD_DIM
DIFF_K_COLS = DIFF_HEADS * 2 * DIFF_HEAD_DIM
DIFF_V_COLS = DIFF_HEADS * 2 * DIFF_HEAD_DIM
GLA_Q_COLS = GLA_HEADS * GLA_DK
GLA_K_COLS = GLA_HEADS * GLA_DK
GLA_V_COLS = GLA_HEADS * GLA_DV
GLA_R_COLS = GLA_HEADS * GLA_DV
GLA_G_COLS = GLA_GATE_RANK
IN_COLS = (DIFF_Q_COLS + DIFF_K_COLS + DIFF_V_COLS + GLA_Q_COLS + GLA_K_COLS
           + GLA_V_COLS + GLA_R_COLS + GLA_G_COLS)

kernel_name = "hybrid_diffattn_gla_peer_adaln"


def rmsnorm(x, g):
    xf = x.astype(jnp.float32)
    y = xf * lax.rsqrt(jnp.mean(xf * xf, axis=-1, keepdims=True) + EPS)
    return (y * g.astype(jnp.float32)).astype(x.dtype)


def rope_cos_sin(positions, dim):
    inv_freq = ROPE_THETA ** (-jnp.arange(0, dim, 2, dtype=jnp.float32) / dim)
    ang = positions.astype(jnp.float32)[..., None] * inv_freq
    return jnp.cos(ang), jnp.sin(ang)


def apply_rope(x, cos, sin):
    x1, x2 = jnp.split(x, 2, axis=-1)
    cos = cos.astype(x.dtype)
    sin = sin.astype(x.dtype)
    return jnp.concatenate([x1 * cos - x2 * sin, x2 * cos + x1 * sin], axis=-1)


def diff_attention(q, k, v, qn_g, kn_g, lam_q1, lam_k1, lam_q2, lam_k2, subln_g, positions, lambda_init):
    B, S, H, _, d = q.shape
    nb = S // Q_BLOCK
    q = rmsnorm(q, qn_g)
    k = rmsnorm(k, kn_g)
    cos, sin = rope_cos_sin(positions, d)
    cos = cos[:, :, None, None, :]
    sin = sin[:, :, None, None, :]
    q = apply_rope(q, cos, sin)
    k = apply_rope(k, cos, sin)
    lam = (jnp.exp(jnp.sum(lam_q1.astype(jnp.float32) * lam_k1.astype(jnp.float32)))
           - jnp.exp(jnp.sum(lam_q2.astype(jnp.float32) * lam_k2.astype(jnp.float32)))
           + lambda_init)
    scale = d ** -0.5
    qb = q.reshape(B, nb, Q_BLOCK, H, 2, d).transpose(1, 0, 3, 4, 2, 5)
    kt = k.transpose(0, 2, 3, 1, 4)
    vt = v.transpose(0, 2, 1, 3)
    key_pos = jnp.arange(S)

    def block(args):
        qblk, i = args
        s = jnp.einsum('bhmqd,bhmkd->bhmqk', qblk, kt,
                       preferred_element_type=jnp.float32) * scale
        q_pos = i * Q_BLOCK + jnp.arange(Q_BLOCK)
        mask = key_pos[None, :] <= q_pos[:, None]
        p = jax.nn.softmax(jnp.where(mask, s, -jnp.inf), axis=-1)
        w = p[:, :, 0] - lam * p[:, :, 1]
        return jnp.einsum('bhqk,bhke->bhqe', w.astype(vt.dtype), vt)

    o = lax.map(block, (qb, jnp.arange(nb)))
    o = o.transpose(1, 0, 3, 2, 4).reshape(B, S, H, 2 * d)
    o = rmsnorm(o, subln_g) * (1.0 - lambda_init)
    return o.reshape(B, S, H * 2 * d)


def gla(q, k, v, r, g_low, w_gate2, b_gate, norm_g):
    B, S, H, dk = q.shape
    dv = v.shape[-1]
    C = GLA_CHUNK
    nc = S // C
    log_a = jax.nn.log_sigmoid((g_low @ w_gate2 + b_gate).astype(jnp.float32)) / GLA_GATE_TAU
    log_a = log_a.reshape(B, S, H, dk)

    def chunks(t):
        return t.reshape(B, nc, C, H, t.shape[-1]).transpose(0, 3, 1, 2, 4).astype(jnp.float32)

    qc = chunks(q) * (dk ** -0.5)
    kc = chunks(k)
    vc = chunks(v)
    bc = jnp.cumsum(chunks(log_a), axis=3)
    b_last = bc[:, :, :, -1:]
    b_mid = bc[:, :, :, C // 2:C // 2 + 1]
    a = jnp.einsum('bhncd,bhnjd->bhncj', qc * jnp.exp(bc - b_mid), kc * jnp.exp(b_mid - bc))
    causal = jnp.tril(jnp.ones((C, C), dtype=bool))
    a = jnp.where(causal, a, 0.0)
    o = jnp.einsum('bhncj,bhnje->bhnce', a, vc)
    ds = jnp.einsum('bhncd,bhnce->bhnde', kc * jnp.exp(b_last - bc), vc)
    decay = jnp.exp(b_last[:, :, :, 0])

    def step(state, inp):
        dec, dsn = inp
        return dec[..., None] * state + dsn, state

    s0 = jnp.zeros((B, H, dk, dv), jnp.float32)
    _, s_prev = lax.scan(step, s0, (jnp.moveaxis(decay, 2, 0), jnp.moveaxis(ds, 2, 0)))
    s_prev = jnp.moveaxis(s_prev, 0, 2)
    o = o + jnp.einsum('bhncd,bhnde->bhnce', qc * jnp.exp(bc), s_prev)
    o = o.transpose(0, 2, 3, 1, 4).reshape(B, S, H, dv)
    o = rmsnorm(o, norm_g).reshape(B, S, H * dv)
    return (o * jax.nn.silu(r.astype(jnp.float32))).astype(q.dtype)


def peer(h, w_query, b_query, keys1, keys2, expert_u, expert_v):
    B, S, D = h.shape
    tokens = h.reshape(-1, PEER_TOKEN_BLOCK, D)
    half = PEER_QDIM // 2

    def block(xb):
        T = xb.shape[0]
        q = (xb @ w_query + b_query).reshape(T, PEER_HEADS, PEER_QDIM)
        s1 = jnp.einsum('thd,kd->thk', q[..., :half], keys1)
        s2 = jnp.einsum('thd,kd->thk', q[..., half:], keys2)
        v1, i1 = lax.top_k(s1, PEER_TOPK)
        v2, i2 = lax.top_k(s2, PEER_TOPK)
        cand = (v1[..., :, None] + v2[..., None, :]).reshape(T, PEER_HEADS, PEER_TOPK * PEER_TOPK)
        cidx = (i1[..., :, None] * PEER_NKEYS + i2[..., None, :]).reshape(T, PEER_HEADS, PEER_TOPK * PEER_TOPK)
        sc, pos = lax.top_k(cand, PEER_TOPK)
        eidx = jnp.take_along_axis(cidx, pos, axis=-1)
        g = jax.nn.softmax(sc.astype(jnp.float32), axis=-1)
        u = expert_u[eidx]
        act = jax.nn.gelu(jnp.einsum('thkd,td->thk', u, xb).astype(jnp.float32), approximate=False) * g
        return jnp.einsum('thk,thkd->td', act.astype(xb.dtype), expert_v[eidx])

    return lax.map(block, tokens).reshape(B, S, D)


def in_proj_offsets():
    sizes = [DIFF_Q_COLS, DIFF_K_COLS, DIFF_V_COLS, GLA_Q_COLS, GLA_K_COLS, GLA_V_COLS, GLA_R_COLS]
    offs = []
    acc = 0
    for s in sizes:
        acc += s
        offs.append(acc)
    return offs


def setup_inputs(seed: int = 0) -> dict:
    key = jax.random.key(seed)
    ks = jax.random.split(key, 26)
    f32 = jnp.float32
    L, D = DEPTH, D_MODEL

    def nrm(k, shape, scale):
        return jax.random.normal(k, shape, f32) * scale

    x = jax.random.normal(ks[0], (BATCH, SEQ, D), f32)
    c = jax.random.normal(ks[1], (BATCH, D), f32)
    positions = (jnp.arange(SEQ, dtype=jnp.int32)[None, :]
                 + jax.random.randint(ks[2], (BATCH, 1), 0, 1024, dtype=jnp.int32))
    return {
        "x": x,
        "c": c,
        "positions": positions,
        "w_ada": nrm(ks[3], (L, D, 6 * D), D ** -0.5),
        "b_ada": nrm(ks[4], (L, 6 * D), 0.01),
        "norm1_g": 1.0 + nrm(ks[5], (L, D), 0.02),
        "w_in": nrm(ks[6], (L, D, IN_COLS), D ** -0.5),
        "qn_g": 1.0 + nrm(ks[7], (L, DIFF_HEAD_DIM), 0.02),
        "kn_g": 1.0 + nrm(ks[8], (L, DIFF_HEAD_DIM), 0.02),
        "lam_q1": nrm(ks[9], (L, DIFF_HEAD_DIM), 0.1),
        "lam_k1": nrm(ks[10], (L, DIFF_HEAD_DIM), 0.1),
        "lam_q2": nrm(ks[11], (L, DIFF_HEAD_DIM), 0.1),
        "lam_k2": nrm(ks[12], (L, DIFF_HEAD_DIM), 0.1),
        "diff_norm_g": 1.0 + nrm(ks[13], (L, 2 * DIFF_HEAD_DIM), 0.02),
        "w_gate2": nrm(ks[14], (L, GLA_GATE_RANK, GLA_HEADS * GLA_DK), GLA_GATE_RANK ** -0.5),
        "b_gate": nrm(ks[15], (L, GLA_HEADS * GLA_DK), 0.01),
        "gla_norm_g": 1.0 + nrm(ks[16], (L, GLA_DV), 0.02),
        "w_out": nrm(ks[17], (L, MIX_WIDTH, D), MIX_WIDTH ** -0.5),
        "norm2_g": 1.0 + nrm(ks[18], (L, D), 0.02),
        "w_query": nrm(ks[19], (L, D, PEER_HEADS * PEER_QDIM), D ** -0.5),
        "b_query": nrm(ks[20], (L, PEER_HEADS * PEER_QDIM), 0.01),
        "peer_keys1": nrm(ks[21], (L, PEER_NKEYS, PEER_QDIM // 2), (PEER_QDIM // 2) ** -0.5),
        "peer_keys2": nrm(ks[22], (L, PEER_NKEYS, PEER_QDIM // 2), (PEER_QDIM // 2) ** -0.5),
        "expert_u": nrm(ks[23], (L, PEER_N_EXPERTS, D), D ** -0.5),
        "expert_v": nrm(ks[24], (L, PEER_N_EXPERTS, D), PEER_HEADS ** -0.5),
    }


def reference(x, c, positions, w_ada, b_ada, norm1_g, w_in, qn_g, kn_g, lam_q1, lam_k1, lam_q2, lam_k2,
              diff_norm_g, w_gate2, b_gate, gla_norm_g, w_out, norm2_g, w_query, b_query,
              peer_keys1, peer_keys2, expert_u, expert_v):
    B, S, D = x.shape
    offs = in_proj_offsets()
    for l in range(DEPTH):
        mod = jax.nn.silu(c) @ w_ada[l] + b_ada[l]
        sh1, sc1, gt1, sh2, sc2, gt2 = [m[:, None, :] for m in jnp.split(mod, 6, axis=-1)]

        h = rmsnorm(x, norm1_g[l]) * (1.0 + sc1) + sh1
        proj = h @ w_in[l]
        dq, dk, dv, gq, gk, gv, gr, gg = jnp.split(proj, offs, axis=-1)
        lambda_init = 0.8 - 0.6 * math.exp(-0.3 * l)
        y_diff = diff_attention(
            dq.reshape(B, S, DIFF_HEADS, 2, DIFF_HEAD_DIM),
            dk.reshape(B, S, DIFF_HEADS, 2, DIFF_HEAD_DIM),
            dv.reshape(B, S, DIFF_HEADS, 2 * DIFF_HEAD_DIM),
            qn_g[l], kn_g[l], lam_q1[l], lam_k1[l], lam_q2[l], lam_k2[l], diff_norm_g[l],
            positions, lambda_init)
        y_gla = gla(
            gq.reshape(B, S, GLA_HEADS, GLA_DK),
            gk.reshape(B, S, GLA_HEADS, GLA_DK),
            gv.reshape(B, S, GLA_HEADS, GLA_DV),
            gr, gg, w_gate2[l], b_gate[l], gla_norm_g[l])
        mixed = jnp.concatenate([y_diff, y_gla], axis=-1) @ w_out[l]
        x = x + gt1 * mixed

        h2 = rmsnorm(x, norm2_g[l]) * (1.0 + sc2) + sh2
        x = x + gt2 * peer(h2, w_query[l], b_query[l], peer_keys1[l], peer_keys2[l], expert_u[l], expert_v[l])
    return x
```

```python
import functools
import math

import numpy as np
import jax
import jax.numpy as jnp
from jax import lax
from jax.experimental import pallas as pl
from jax.experimental.pallas import tpu as pltpu

F32 = jnp.float32
BF16 = jnp.bfloat16
EPS = 1e-6
NEG = -1e30

DIFF_HEADS = 4
DIFF_HEAD_DIM = 64
ROPE_THETA = 10000.0
GLA_HEADS = 4
GLA_DK = 64
GLA_DV = 128
GLA_GATE_TAU = 16.0
GLA_CHUNK = 64
PEER_HEADS = 8
PEER_TOPK = 16
PEER_NKEYS = 128

LANES = 128
SUBLANES = 8
VMEM_LIMIT = 56 * 1024 * 1024

DIFF_W = DIFF_HEADS * 2 * DIFF_HEAD_DIM
GLA_QK_W = GLA_HEADS * GLA_DK
GLA_V_W = GLA_HEADS * GLA_DV


def _params(sem):
    return pltpu.CompilerParams(dimension_semantics=sem, vmem_limit_bytes=VMEM_LIMIT)


def _dot(a, b):
    return jnp.dot(a, b, preferred_element_type=F32)


def _dot_nt(a, b):
    return lax.dot_general(a, b, (((1,), (1,)), ((), ())), preferred_element_type=F32)


def _dot_tn(a, b):
    return lax.dot_general(a, b, (((0,), (0,)), ((), ())), preferred_element_type=F32)


def _split(x):
    hi = x.astype(BF16)
    lo = (x - hi.astype(F32)).astype(BF16)
    return hi, lo


def _dot3(a, b):
    ah, al = _split(a)
    bh, bl = _split(b)
    return _dot(ah, bh) + _dot(al, bh) + _dot(ah, bl)


def _adaln_kernel(c_ref, w_ref, b_ref, o_ref):
    c = c_ref[...]
    a = c * jax.nn.sigmoid(c)
    o_ref[...] = _dot3(a, w_ref[...]) + b_ref[...]


def _adaln(c, w_ada, b_ada):
    B, D = c.shape
    n = w_ada.shape[1] // D
    out = pl.pallas_call(
        _adaln_kernel,
        grid=(n,),
        in_specs=[pl.BlockSpec((B, D), lambda j: (0, 0)),
                  pl.BlockSpec((D, D), lambda j: (0, j)),
                  pl.BlockSpec((1, D), lambda j: (0, j))],
        out_specs=pl.BlockSpec((B, D), lambda j: (0, j)),
        out_shape=jax.ShapeDtypeStruct((B, n * D), F32),
        compiler_params=_params(("parallel",)),
        name="adaln",
    )(c, w_ada, b_ada.reshape(1, n * D))
    return out.reshape(B, n, D)


def _rope_kernel(pos_ref, invf_ref, sgn_ref, cos_ref, sin_ref):
    ang = pos_ref[0].astype(F32) * invf_ref[...]
    cos_ref[0] = jnp.cos(ang)
    sin_ref[0] = jnp.sin(ang) * sgn_ref[...]


def _rope_tables(positions, ts):
    B, S = positions.shape
    d = DIFF_HEAD_DIM
    inv_freq = ROPE_THETA ** (-jnp.arange(0, d, 2, dtype=F32) / d)
    invf = jnp.tile(inv_freq, LANES // (d // 2)).reshape(1, LANES)
    sgn = jnp.tile(jnp.concatenate([-jnp.ones(d // 2, F32), jnp.ones(d // 2, F32)]),
                   LANES // d).reshape(1, LANES)
    spec = pl.BlockSpec((1, ts, LANES), lambda b, i: (b, i, 0))
    return pl.pallas_call(
        _rope_kernel,
        grid=(B, S // ts),
        in_specs=[pl.BlockSpec((1, ts, 1), lambda b, i: (b, i, 0)),
                  pl.BlockSpec((1, LANES), lambda b, i: (0, 0)),
                  pl.BlockSpec((1, LANES), lambda b, i: (0, 0))],
        out_specs=[spec, spec],
        out_shape=[jax.ShapeDtypeStruct((B, S, LANES), F32)] * 2,
        compiler_params=_params(("parallel", "parallel")),
        name="rope",
    )(positions.reshape(B, S, 1), invf, sgn)


def _inproj_kernel(x_ref, mod_ref, g1_ref, w_ref, cos_ref, sin_ref, qg_ref, kg_ref, bd_ref,
                   qd_ref, kd_ref, vd_ref, gq_ref, gk_ref, gv_ref, gr_ref, gg_ref):
    x = x_ref[0]
    ms = jnp.mean(x * x, axis=-1, keepdims=True)
    h = x * lax.rsqrt(ms + EPS) * g1_ref[...]
    h = h * (1.0 + mod_ref[0, 1:2, :]) + mod_ref[0, 0:1, :]
    hb = h.astype(BF16)

    def proj(a, b):
        return _dot(hb, w_ref[:, a:b])

    reps = DIFF_W // LANES
    cos = jnp.concatenate([cos_ref[0]] * reps, axis=1)
    sin = jnp.concatenate([sin_ref[0]] * reps, axis=1)
    lane = lax.broadcasted_iota(jnp.int32, (1, DIFF_W), 1)
    lower = (lane & (DIFF_HEAD_DIM // 2)) == 0
    half = DIFF_HEAD_DIM // 2

    def norm_rope(t, g, scale):
        ss = _dot((t * t).astype(BF16), bd_ref[...])
        tn = t * lax.rsqrt(ss + EPS) * g
        partner = jnp.where(lower, pltpu.roll(tn, DIFF_W - half, 1), pltpu.roll(tn, half, 1))
        return (tn * cos + partner * sin) * scale

    o = 0
    qd_ref[0] = norm_rope(proj(o, o + DIFF_W), qg_ref[...], DIFF_HEAD_DIM ** -0.5).astype(BF16)
    o += DIFF_W
    kd_ref[0] = norm_rope(proj(o, o + DIFF_W), kg_ref[...], 1.0).astype(BF16)
    o += DIFF_W
    vd_ref[0] = proj(o, o + DIFF_W).astype(BF16)
    o += DIFF_W
    gq_ref[0] = proj(o, o + GLA_QK_W).astype(BF16)
    o += GLA_QK_W
    gk_ref[0] = proj(o, o + GLA_QK_W).astype(BF16)
    o += GLA_QK_W
    gv_ref[0] = proj(o, o + GLA_V_W).astype(BF16)
    o += GLA_V_W
    gr_ref[0] = proj(o, o + GLA_V_W).astype(BF16)
    o += GLA_V_W
    gg_ref[0] = proj(o, w_ref.shape[1])


def _inproj(x, mod, g1, w_in, cos_t, sin_t, qn_g, kn_g, tm):
    B, S, D = x.shape
    cols = w_in.shape[1]
    rank = cols - (3 * DIFF_W + 2 * GLA_QK_W + 2 * GLA_V_W)
    d = DIFF_HEAD_DIM
    grp = np.arange(DIFF_W) // d
    bd = jnp.asarray((grp[:, None] == grp[None, :]).astype(np.float32) / d, BF16)
    qg = jnp.tile(qn_g, DIFF_W // d).reshape(1, DIFF_W)
    kg = jnp.tile(kn_g, DIFF_W // d).reshape(1, DIFF_W)

    def tok(w):
        return pl.BlockSpec((1, tm, w), lambda b, i: (b, i, 0))

    def const(shape):
        return pl.BlockSpec(shape, lambda b, i: (0,) * len(shape))

    widths = [DIFF_W, DIFF_W, DIFF_W, GLA_QK_W, GLA_QK_W, GLA_V_W, GLA_V_W]
    out_shape = [jax.ShapeDtypeStruct((B, S, w), BF16) for w in widths]
    out_shape.append(jax.ShapeDtypeStruct((B, S, rank), F32))
    return pl.pallas_call(
        _inproj_kernel,
        grid=(B, S // tm),
        in_specs=[tok(D),
                  pl.BlockSpec((1, mod.shape[1], D), lambda b, i: (b, 0, 0)),
                  const((1, D)), const((D, cols)), tok(LANES), tok(LANES),
                  const((1, DIFF_W)), const((1, DIFF_W)), const((DIFF_W, DIFF_W))],
        out_specs=[tok(w) for w in widths] + [tok(rank)],
        out_shape=out_shape,
        compiler_params=_params(("parallel", "parallel")),
        name="inproj",
    )(x, mod, g1.reshape(1, D), w_in.astype(BF16), cos_t, sin_t, qg, kg, bd)


def _attn_kernel(q_ref, k_ref, v_ref, lam_ref, sg_ref, o_ref, *, tq, lambda_init):
    i = pl.program_id(1)
    hw = 2 * DIFF_HEAD_DIM
    lam_p = lam_ref[...]
    lam = (jnp.exp(jnp.sum(lam_p[0:1] * lam_p[1:2], axis=-1, keepdims=True))
           - jnp.exp(jnp.sum(lam_p[2:3] * lam_p[3:4], axis=-1, keepdims=True)) + lambda_init)
    lane = lax.broadcasted_iota(jnp.int32, (1, hw), 1)
    first_map = lane < DIFF_HEAD_DIM
    row = lax.broadcasted_iota(jnp.int32, (2 * tq, tq), 0)
    col = lax.broadcasted_iota(jnp.int32, (2 * tq, tq), 1)
    causal = col <= jnp.where(row >= tq, row - tq, row)

    for h in range(DIFF_HEADS):
        cs = slice(h * hw, (h + 1) * hw)
        q = q_ref[0, :, cs]
        zero = jnp.zeros_like(q)
        q2 = jnp.concatenate([jnp.where(first_map, q, zero), jnp.where(first_map, zero, q)], axis=0)

        def block(j, carry, masked):
            m, l, acc = carry
            r0 = pl.multiple_of(j * tq, tq)
            kb = k_ref[0, pl.ds(r0, tq), cs]
            vb = v_ref[0, pl.ds(r0, tq), cs]
            s = _dot_nt(q2, kb)
            if masked:
                s = jnp.where(causal, s, NEG)
            m_new = jnp.maximum(m, jnp.max(s, axis=-1, keepdims=True))
            alpha = jnp.exp(m - m_new)
            p = jnp.exp(s - m_new)
            l = alpha * l + jnp.sum(p, axis=-1, keepdims=True)
            acc = alpha * acc + _dot(p.astype(BF16), vb)
            return m_new, l, acc

        init = (jnp.full((2 * tq, 1), NEG, F32), jnp.zeros((2 * tq, 1), F32), jnp.zeros((2 * tq, hw), F32))
        carry = lax.fori_loop(0, i, lambda j, c: block(j, c, False), init)
        _, l, acc = block(i, carry, True)
        on = acc / l
        o = on[:tq] - lam * on[tq:]
        ms = jnp.mean(o * o, axis=-1, keepdims=True)
        o = o * lax.rsqrt(ms + EPS) * sg_ref[...] * (1.0 - lambda_init)
        o_ref[0, :, cs] = o.astype(BF16)


def _attn(qd, kd, vd, lam_p, subln_g, lambda_init, tq):
    B, S, W = qd.shape
    kv = pl.BlockSpec((1, S, W), lambda b, i: (b, 0, 0))
    return pl.pallas_call(
        functools.partial(_attn_kernel, tq=tq, lambda_init=lambda_init),
        grid=(B, S // tq),
        in_specs=[pl.BlockSpec((1, tq, W), lambda b, i: (b, i, 0)), kv, kv,
                  pl.BlockSpec(lam_p.shape, lambda b, i: (0, 0)),
                  pl.BlockSpec((1, 2 * DIFF_HEAD_DIM), lambda b, i: (0, 0))],
        out_specs=pl.BlockSpec((1, tq, W), lambda b, i: (b, i, 0)),
        out_shape=jax.ShapeDtypeStruct((B, S, W), BF16),
        compiler_params=_params(("parallel", "arbitrary")),
        name="attn",
    )(qd, kd, vd, lam_p, subln_g.reshape(1, 2 * DIFF_HEAD_DIM))


def _gla_kernel(q_ref, k_ref, v_ref, r_ref, gg_ref, wg_ref, bg_ref, ng_ref, tri_ref, o_ref,
                bc_ref, st_ref):
    S = q_ref.shape[1]
    C = GLA_CHUNK
    grp = tri_ref.shape[0]

    def cum_body(g, _):
        r0 = pl.multiple_of(g * grp, grp)
        z = _dot3(gg_ref[0, pl.ds(r0, grp), :], wg_ref[...]) + bg_ref[...]
        la = (jnp.minimum(z, 0.0) - jnp.log(1.0 + jnp.exp(-jnp.abs(z)))) * (1.0 / GLA_GATE_TAU)
        hi, lo = _split(la)
        bc_ref[pl.ds(r0, grp), :] = _dot(tri_ref[...], hi) + _dot(tri_ref[...], lo)
        return 0

    lax.fori_loop(0, S // grp, cum_body, 0)

    st_ref[...] = jnp.zeros_like(st_ref)
    hq = GLA_HEADS * C
    sh_c, sh_dk, sh_dv = C.bit_length() - 1, GLA_DK.bit_length() - 1, GLA_DV.bit_length() - 1
    rowq = lax.broadcasted_iota(jnp.int32, (hq, GLA_QK_W), 0)
    laneq = lax.broadcasted_iota(jnp.int32, (hq, GLA_QK_W), 1)
    head_sel = (rowq >> sh_c) == (laneq >> sh_dk)
    rowa = lax.broadcasted_iota(jnp.int32, (hq, C), 0)
    cola = lax.broadcasted_iota(jnp.int32, (hq, C), 1)
    causal = cola <= (rowa & (C - 1))
    rows = lax.broadcasted_iota(jnp.int32, (GLA_QK_W, GLA_V_W), 0)
    lanes = lax.broadcasted_iota(jnp.int32, (GLA_QK_W, GLA_V_W), 1)
    state_sel = (rows >> sh_dk) == (lanes >> sh_dv)
    scale = GLA_DK ** -0.5

    def chunk_body(n, _):
        r0 = pl.multiple_of(n * C, C)
        bc = bc_ref[pl.ds(r0, C), :]
        b_mid = bc[C // 2:C // 2 + 1]
        q = q_ref[0, pl.ds(r0, C), :].astype(F32) * scale
        k = k_ref[0, pl.ds(r0, C), :].astype(F32)
        v = v_ref[0, pl.ds(r0, C), :]
        q_in = (q * jnp.exp(bc - b_mid)).astype(BF16)
        k_in = (k * jnp.exp(b_mid - bc)).astype(BF16)
        q_st = (q * jnp.exp(bc)).astype(BF16)
        bct = bc.T
        bt_last = bct[:, C - 1:C]
        kt_st = (k.T * jnp.exp(bt_last - bct)).astype(BF16)

        q_heads = jnp.where(head_sel, jnp.concatenate([q_in] * GLA_HEADS, axis=0), jnp.zeros((), BF16))
        a = _dot_nt(q_heads, k_in)
        a = jnp.where(causal, a, 0.0).astype(BF16)
        full = _dot(a, v)
        o = jnp.concatenate([full[h * C:(h + 1) * C, h * GLA_DV:(h + 1) * GLA_DV]
                             for h in range(GLA_HEADS)], axis=1)

        st = st_ref[...]
        o = o + _dot(q_st, st.astype(BF16))
        upd = _dot(kt_st, v)
        st_ref[...] = st * jnp.exp(bt_last) + jnp.where(state_sel, upd, 0.0)

        r = r_ref[0, pl.ds(r0, C), :].astype(F32)
        gate = r * jax.nn.sigmoid(r)
        outs = []
        for h in range(GLA_HEADS):
            oh = o[:, h * GLA_DV:(h + 1) * GLA_DV]
            ms = jnp.mean(oh * oh, axis=-1, keepdims=True)
            outs.append(oh * lax.rsqrt(ms + EPS) * ng_ref[...])
        o_ref[0, pl.ds(r0, C), :] = (jnp.concatenate(outs, axis=1) * gate).astype(BF16)
        return 0

    lax.fori_loop(0, S // C, chunk_body, 0)


def _gla(gq, gk, gv, gr, gg, w_gate2, b_gate, norm_g):
    B, S, _ = gq.shape
    grp = 2 * GLA_CHUNK
    idx = np.arange(grp)
    tri = jnp.asarray(((idx[:, None] // GLA_CHUNK == idx[None, :] // GLA_CHUNK)
                       & (idx[None, :] <= idx[:, None])).astype(np.float32), BF16)

    def seq(w):
        return pl.BlockSpec((1, S, w), lambda b: (b, 0, 0))

    def const(shape):
        return pl.BlockSpec(shape, lambda b: (0,) * len(shape))

    rank = gg.shape[-1]
    return pl.pallas_call(
        _gla_kernel,
        grid=(B,),
        in_specs=[seq(GLA_QK_W), seq(GLA_QK_W), seq(GLA_V_W), seq(GLA_V_W), seq(rank),
                  const((rank, GLA_QK_W)), const((1, GLA_QK_W)), const((1, GLA_DV)), const((grp, grp))],
        out_specs=seq(GLA_V_W),
        out_shape=jax.ShapeDtypeStruct((B, S, GLA_V_W), BF16),
        scratch_shapes=[pltpu.VMEM((S, GLA_QK_W), F32), pltpu.VMEM((GLA_QK_W, GLA_V_W), F32)],
        compiler_params=_params(("parallel",)),
        name="gla",
    )(gq, gk, gv, gr, gg, w_gate2, b_gate.reshape(1, GLA_QK_W), norm_g.reshape(1, GLA_DV), tri)


def _postmix_kernel(yd_ref, yg_ref, x_ref, mod_ref, g2_ref, wo_ref, wq_ref, bq_ref, k1_ref, k2_ref,
                    x1_ref, h2t_ref, s1_ref, s2_ref):
    half = wo_ref.shape[0] // 2
    mixed = _dot(yd_ref[0], wo_ref[:half, :]) + _dot(yg_ref[0], wo_ref[half:, :])
    x1 = x_ref[0] + mod_ref[0, 2:3, :] * mixed
    x1_ref[0] = x1
    ms = jnp.mean(x1 * x1, axis=-1, keepdims=True)
    h2 = x1 * lax.rsqrt(ms + EPS) * g2_ref[...]
    h2 = h2 * (1.0 + mod_ref[0, 4:5, :]) + mod_ref[0, 3:4, :]
    h2t = h2.T.astype(BF16)
    h2t_ref[...] = h2t
    qt = (_dot(wq_ref[...], h2t) + bq_ref[...]).astype(BF16)
    nk = PEER_NKEYS
    for h in range(PEER_HEADS):
        s1_ref[h] = _dot(k1_ref[...], qt[(2 * h) * nk:(2 * h + 1) * nk])
        s2_ref[h] = _dot(k2_ref[...], qt[(2 * h + 1) * nk:(2 * h + 2) * nk])


def _postmix(yd, yg, x, mod, g2, w_out, w_query, b_query, keys1, keys2, tm):
    B, S, D = x.shape
    N = B * S
    nq = w_query.shape[1]
    nb = S // tm

    def tok(w):
        return pl.BlockSpec((1, tm, w), lambda b, i: (b, i, 0))

    def const(shape):
        return pl.BlockSpec(shape, lambda b, i: (0,) * len(shape))

    sspec = pl.BlockSpec((PEER_HEADS, PEER_NKEYS, tm), lambda b, i: (0, 0, b * nb + i))
    return pl.pallas_call(
        _postmix_kernel,
        grid=(B, nb),
        in_specs=[tok(yd.shape[-1]), tok(yg.shape[-1]), tok(D),
                  pl.BlockSpec((1, mod.shape[1], D), lambda b, i: (b, 0, 0)),
                  const((1, D)), const((D, D)), const((nq, D)), const((nq, 1)),
                  const(keys1.shape), const(keys2.shape)],
        out_specs=[tok(D), pl.BlockSpec((D, tm), lambda b, i: (0, b * nb + i)), sspec, sspec],
        out_shape=[jax.ShapeDtypeStruct((B, S, D), F32),
                   jax.ShapeDtypeStruct((D, N), BF16),
                   jax.ShapeDtypeStruct((PEER_HEADS, PEER_NKEYS, N), F32),
                   jax.ShapeDtypeStruct((PEER_HEADS, PEER_NKEYS, N), F32)],
        compiler_params=_params(("parallel", "parallel")),
        name="postmix",
    )(yd, yg, x, mod, g2.reshape(1, D), w_out.astype(BF16), w_query.T.astype(BF16),
      b_query.reshape(nq, 1), keys1.astype(BF16), keys2.astype(BF16))


def _select(work, ids, count):
    order = jnp.full(work.shape, float(count), F32)
    vals = jnp.zeros((count, work.shape[1]), F32)
    slot = lax.broadcasted_iota(jnp.int32, vals.shape, 0)
    for r in range(count):
        m = jnp.max(work, axis=0, keepdims=True)
        first = jnp.min(jnp.where(work == m, ids, 1e9), axis=0, keepdims=True)
        sel = ids == first
        order = jnp.where(sel, float(r), order)
        work = jnp.where(sel, NEG, work)
        vals = jnp.where(slot == r, m, vals)
    return order, vals


def _stair_rows():
    k, g = PEER_TOPK, SUBLANES
    rows = [(0, r2) for r2 in range(k)]
    for r1 in range(1, g):
        rows += [(r1, r2) for r2 in range(g)]
    rows += [(r1, 0) for r1 in range(g, k)]
    return rows


def _topk_kernel(s1_ref, s2_ref, ids_ref, c_ref, p_ref, d_ref, q_ref):
    nk, k, g = PEER_NKEYS, PEER_TOPK, SUBLANES
    tiles = s1_ref.shape[2] // LANES
    key_ids = lax.broadcasted_iota(jnp.int32, (nk, LANES), 0).astype(F32)
    cand_ids = ids_ref[...]

    def unit(u, _):
        h = u // tiles
        t0 = pl.multiple_of((u % tiles) * LANES, LANES)
        s1 = s1_ref[h, :, pl.ds(t0, LANES)]
        s2 = s2_ref[h, :, pl.ds(t0, LANES)]
        rank1, v1 = _select(s1, key_ids, k)
        rank2, v2 = _select(s2, key_ids, k)
        blocks = [v2 + v1[0:1]]
        for r1 in range(1, g):
            blocks.append(v2[0:g] + v1[r1:r1 + 1])
        blocks.append(v1[g:k] + v2[0:1])
        cand = jnp.concatenate(blocks, axis=0)
        order, _ = _select(cand, cand_ids, k)
        sel = order < k
        e = jnp.where(sel, jnp.exp(cand - cand[0:1]), 0.0)
        zinv = 1.0 / jnp.sum(e, axis=0, keepdims=True)
        self32 = sel.astype(F32)
        limit = jnp.zeros((nk, LANES), F32)
        cnt0 = jnp.sum(self32[0:k], axis=0, keepdims=True)
        limit = jnp.where(rank1 == 0, cnt0, limit)
        for r1 in range(1, g):
            lo = k + (r1 - 1) * g
            cnt = jnp.sum(self32[lo:lo + g], axis=0, keepdims=True)
            limit = jnp.where(rank1 == r1, cnt, limit)
        for r1 in range(g, k):
            lo = k + (g - 1) * g + (r1 - g)
            limit = jnp.where(rank1 == r1, self32[lo:lo + 1], limit)
        gate1 = jnp.where(rank1 < k, jnp.exp(s1 - v1[0:1]) * zinv, 0.0)
        gate2 = jnp.exp(s2 - v2[0:1])
        c_ref[h, :, pl.ds(t0, LANES)] = gate1.astype(BF16)
        p_ref[h, :, pl.ds(t0, LANES)] = limit.astype(BF16)
        d_ref[h, :, pl.ds(t0, LANES)] = gate2.astype(BF16)
        q_ref[h, :, pl.ds(t0, LANES)] = rank2.astype(BF16)
        return 0

    lax.fori_loop(0, PEER_HEADS * tiles, unit, 0)


def _topk(s1t, s2t, tt):
    H, nk, N = s1t.shape
    rows = _stair_rows()
    ids = np.asarray([r1 * PEER_TOPK + r2 for r1, r2 in rows], np.float32)
    ids = jnp.asarray(np.broadcast_to(ids[:, None], (len(rows), LANES)))
    spec = pl.BlockSpec((H, nk, tt), lambda i: (0, 0, i))
    return pl.pallas_call(
        _topk_kernel,
        grid=(N // tt,),
        in_specs=[spec, spec, pl.BlockSpec(ids.shape, lambda i: (0, 0))],
        out_specs=[spec] * 4,
        out_shape=[jax.ShapeDtypeStruct((H, nk, N), BF16)] * 4,
        compiler_params=_params(("parallel",)),
        name="topk",
    )(s1t, s2t, ids)


def _peer_kernel(h2t_ref, u_ref, vt_ref, c_ref, p_ref, d_ref, q_ref, x1_ref, mod_ref, o_ref,
                 acc_ref, z_ref, act_ref, cf_ref, pf_ref, df_ref, qf_ref):
    e = pl.program_id(1)
    nk = PEER_NKEYS
    groups = u_ref.shape[0] // nk

    @pl.when(e == 0)
    def _():
        acc_ref[...] = jnp.zeros_like(acc_ref)
        df_ref[...] = d_ref[...].astype(F32)
        qf_ref[...] = q_ref[...].astype(F32)

    cf_ref[...] = c_ref[...].astype(F32)
    pf_ref[...] = p_ref[...].astype(F32)
    z_ref[...] = _dot(u_ref[...], h2t_ref[...])

    def group(j, _):
        r0 = pl.multiple_of(j * nk, nk)
        z = z_ref[pl.ds(r0, nk), :]
        w = jnp.zeros_like(z)
        for h in range(PEER_HEADS):
            lim = pf_ref[h, pl.ds(j, 1), :]
            g1 = cf_ref[h, pl.ds(j, 1), :]
            w = w + jnp.where(qf_ref[h] < lim, g1 * df_ref[h], 0.0)
        gelu = 0.5 * z * (1.0 + lax.erf(z * (2.0 ** -0.5)))
        act_ref[pl.ds(r0, nk), :] = (gelu * w).astype(BF16)
        return 0

    lax.fori_loop(0, groups, group, 0)
    acc_ref[...] += _dot(vt_ref[...], act_ref[...])

    @pl.when(e == pl.num_programs(1) - 1)
    def _():
        o_ref[0] = x1_ref[0] + mod_ref[0, 5:6, :] * acc_ref[...].T


def _peer(h2t, x1, mod, cpdq, expert_u, expert_v, tt, te):
    B, S, D = x1.shape
    N = B * S
    E = expert_u.shape[0]
    H, nk = PEER_HEADS, PEER_NKEYS
    gpc = te // nk
    nb = S // tt
    c, p, d, q = cpdq
    row_spec = pl.BlockSpec((H, gpc, tt), lambda i, e: (0, e, i))
    key_spec = pl.BlockSpec((H, nk, tt), lambda i, e: (0, 0, i))
    tok = pl.BlockSpec((1, tt, D), lambda i, e: (i // nb, i % nb, 0))
    return pl.pallas_call(
        _peer_kernel,
        grid=(N // tt, E // te),
        in_specs=[pl.BlockSpec((D, tt), lambda i, e: (0, i)),
                  pl.BlockSpec((te, D), lambda i, e: (e, 0)),
                  pl.BlockSpec((D, te), lambda i, e: (0, e)),
                  row_spec, row_spec, key_spec, key_spec, tok,
                  pl.BlockSpec((1, mod.shape[1], D), lambda i, e: (i // nb, 0, 0))],
        out_specs=tok,
        out_shape=jax.ShapeDtypeStruct((B, S, D), F32),
        scratch_shapes=[pltpu.VMEM((D, tt), F32), pltpu.VMEM((te, tt), F32), pltpu.VMEM((te, tt), BF16),
                        pltpu.VMEM((H, gpc, tt), F32), pltpu.VMEM((H, gpc, tt), F32),
                        pltpu.VMEM((H, nk, tt), F32), pltpu.VMEM((H, nk, tt), F32)],
        compiler_params=_params(("parallel", "arbitrary")),
        name="peer",
    )(h2t, expert_u.astype(BF16), expert_v.T.astype(BF16), c, p, d, q, x1, mod)


def _tile(n, want):
    t = min(n, want)
    while n % t:
        t //= 2
    return t


def kernel(x, c, positions, w_ada, b_ada, norm1_g, w_in, qn_g, kn_g, lam_q1, lam_k1, lam_q2, lam_k2,
           diff_norm_g, w_gate2, b_gate, gla_norm_g, w_out, norm2_g, w_query, b_query,
           peer_keys1, peer_keys2, expert_u, expert_v):
    B, S, D = x.shape
    depth = w_ada.shape[0]
    cos_t, sin_t = _rope_tables(positions, _tile(S, 512))
    for l in range(depth):
        mod = _adaln(c, w_ada[l], b_ada[l])
        qd, kd, vd, gq, gk, gv, gr, gg = _inproj(x, mod, norm1_g[l], w_in[l], cos_t, sin_t,
                                                 qn_g[l], kn_g[l], _tile(S, 512))
        lambda_init = 0.8 - 0.6 * math.exp(-0.3 * l)
        lam_p = jnp.stack([lam_q1[l], lam_k1[l], lam_q2[l], lam_k2[l]])
        yd = _attn(qd, kd, vd, lam_p, diff_norm_g[l], lambda_init, _tile(S, 256))
        yg = _gla(gq, gk, gv, gr, gg, w_gate2[l], b_gate[l], gla_norm_g[l])
        x1, h2t, s1t, s2t = _postmix(yd, yg, x, mod, norm2_g[l], w_out[l], w_query[l], b_query[l],
                                     peer_keys1[l], peer_keys2[l], _tile(S, 256))
        cpdq = _topk(s1t, s2t, _tile(B * S, 512))
        x = _peer(h2t, x1, mod, cpdq, expert_u[l], expert_v[l], _tile(S, 512), 2048)
    return x
```
